```python
import math
import jax
import jax.numpy as jnp
from jax import lax
import numpy as np

D_MODEL = 1024
BATCH = 8
SEQ = 4096
DEPTH = 1
DEC_BATCH = 128
DEC_SEQ = 4
PAST_LEN = 16384
PAGE_SIZE = 128

M_HEADS = 4
M_HEAD_DIM = D_MODEL // M_HEADS
M_WIDTH = M_HEADS * M_HEAD_DIM
M_CHUNK = 128
A_HEADS = 16
A_KV_HEADS = 4
A_GROUP = A_HEADS // A_KV_HEADS
A_HEAD_DIM = D_MODEL // A_HEADS
A_Q_WIDTH = A_HEADS * A_HEAD_DIM
A_KV_WIDTH = A_KV_HEADS * A_HEAD_DIM
WINDOW = 128
ROPE_DIM = A_HEAD_DIM // 4
ROPE_THETA = 500000.0
N_GROUPS = 4
EXPERTS_PER_GROUP = 8
N_EXPERTS = N_GROUPS * EXPERTS_PER_GROUP
TOP_K = 2
EXPERT_FF = D_MODEL // 4
MOE_BLOCK = 128
EPS = 1e-6
SPLITS = (M_WIDTH, M_WIDTH, M_WIDTH, M_WIDTH, 2 * M_HEADS, A_Q_WIDTH, A_KV_WIDTH, A_KV_WIDTH, D_MODEL, D_MODEL)
IN_WIDTH = sum(SPLITS)

kernel_name = 'hybrid_mlstm_swa_hmoe_step'


def _rmsnorm(x, g):
    xf = x.astype(jnp.float32)
    y = xf * lax.rsqrt(jnp.mean(xf * xf, axis=-1, keepdims=True) + EPS)
    return (y * g.astype(jnp.float32)).astype(x.dtype)


def _rope_partial(x, pos):
    half = ROPE_DIM // 2
    inv_freq = jnp.power(jnp.float32(ROPE_THETA), -jnp.arange(0, ROPE_DIM, 2, dtype=jnp.float32) / ROPE_DIM)
    ang = pos.astype(jnp.float32)[:, None] * inv_freq[None, :]
    cos = jnp.cos(ang)[:, None, :]
    sin = jnp.sin(ang)[:, None, :]
    xf = x.astype(jnp.float32)
    x1 = xf[..., :half]
    x2 = xf[..., half:ROPE_DIM]
    out = jnp.concatenate([x1 * cos - x2 * sin, x2 * cos + x1 * sin, xf[..., ROPE_DIM:]], axis=-1)
    return out.astype(x.dtype)


def _sink_attention(q, k, v, qpos, kpos, sinks):
    s = jnp.einsum('...qkgd,...skd->...kgqs', q, k).astype(jnp.float32) * (A_HEAD_DIM ** -0.5)
    rel = qpos[..., :, None] - kpos[..., None, :]
    mask = (rel >= 0) & (rel < WINDOW) & (kpos[..., None, :] >= 0)
    s = jnp.where(mask[..., None, None, :, :], s, -jnp.inf)
    sink = sinks.astype(jnp.float32).reshape(A_KV_HEADS, A_GROUP)[:, :, None]
    m = jnp.maximum(jnp.max(s, axis=-1), sink)
    p = jnp.exp(s - m[..., None])
    den = jnp.sum(p, axis=-1) + jnp.exp(sink - m)
    probs = (p / den[..., None]).astype(v.dtype)
    return jnp.einsum('...kgqs,...skd->...qkgd', probs, v)


def _swa_prompt(q, k, v, sinks):
    b, t = q.shape[:2]
    nb = t // WINDOW
    qb = q.reshape(b, nb, WINDOW, A_KV_HEADS, A_GROUP, A_HEAD_DIM)
    kb = k.reshape(b, nb, WINDOW, A_KV_HEADS, A_HEAD_DIM)
    vb = v.reshape(b, nb, WINDOW, A_KV_HEADS, A_HEAD_DIM)

    def with_prev(a):
        prev = jnp.pad(a[:, :-1], ((0, 0), (1, 0), (0, 0), (0, 0), (0, 0)))
        return jnp.concatenate([prev, a], axis=2)

    pos = jnp.arange(t, dtype=jnp.int32).reshape(nb, WINDOW)
    kpos = jnp.concatenate([pos - WINDOW, pos], axis=-1)
    o = _sink_attention(qb, with_prev(kb), with_prev(vb), pos, kpos, sinks)
    return o.reshape(b, t, A_Q_WIDTH)


def _mlstm(q, k, v, ig, logf, c0, n0, m0):
    b, t, h, d = q.shape
    L = math.gcd(t, M_CHUNK)
    nc = t // L

    def chunks(a):
        a = jnp.moveaxis(a, 2, 1).astype(jnp.float32)
        return jnp.moveaxis(a.reshape(a.shape[:2] + (nc, L) + a.shape[3:]), 2, 0)

    causal = jnp.tril(jnp.ones((L, L), dtype=bool))

    def step(carry, xs):
        c, n, m = carry
        qc, kc, vc, ic, fc = xs
        bcum = jnp.cumsum(fc, axis=-1)
        dlog = jnp.where(causal, bcum[..., :, None] - bcum[..., None, :] + ic[..., None, :], -jnp.inf)
        mt = jnp.maximum(bcum + m[..., None], jnp.max(dlog, axis=-1))
        dmat = jnp.exp(dlog - mt[..., None])
        sc = jnp.exp(bcum + m[..., None] - mt)
        s = jnp.einsum('bhld,bhsd->bhls', qc, kc) * dmat
        num = jnp.einsum('bhls,bhsd->bhld', s, vc) + sc[..., None] * jnp.einsum('bhld,bhde->bhle', qc, c)
        den = jnp.sum(s, axis=-1) + sc * jnp.einsum('bhld,bhd->bhl', qc, n)
        hout = num / jnp.maximum(jnp.abs(den), jnp.exp(-mt))[..., None]
        m_end = mt[..., -1]
        ws = jnp.exp(bcum[..., -1:] - bcum + ic - m_end[..., None])
        wc = jnp.exp(bcum[..., -1] + m - m_end)
        kw = kc * ws[..., None]
        c_new = wc[..., None, None] * c + jnp.einsum('bhsd,bhse->bhde', kw, vc)
        n_new = wc[..., None] * n + jnp.sum(kw, axis=-2)
        return (c_new, n_new, m_end), hout

    init = (c0.astype(jnp.float32), n0.astype(jnp.float32), m0.astype(jnp.float32))
    (c, n, m), hs = lax.scan(step, init, (chunks(q), chunks(k), chunks(v), chunks(ig), chunks(logf)))
    hs = jnp.moveaxis(jnp.moveaxis(hs, 0, 2).reshape(b, h, t, d), 1, 2)
    return hs, c, n, m


def _hier_moe(hx, w_group, b_group, w_router, b_router, w_gate_e, w_up_e, w_down_e):
    n_tok, d = hx.shape
    g_logits = (hx @ w_group).astype(jnp.float32) + b_group.astype(jnp.float32)
    p_group = jax.nn.softmax(g_logits, axis=-1)
    grp = jnp.argmax(g_logits, axis=-1)
    e_logits = ((hx @ w_router).astype(jnp.float32) + b_router.astype(jnp.float32)).reshape(n_tok, N_GROUPS, EXPERTS_PER_GROUP)
    in_grp = jnp.take_along_axis(e_logits, grp[:, None, None], axis=1)[:, 0]
    top_v, top_i = lax.top_k(in_grp, TOP_K)
    gate = jax.nn.softmax(top_v, axis=-1) * jnp.take_along_axis(p_group, grp[:, None], axis=1)
    flat_e = (grp[:, None] * EXPERTS_PER_GROUP + top_i).reshape(-1).astype(jnp.int32)
    flat_w = gate.reshape(-1)
    n_asg = n_tok * TOP_K
    flat_tok = jnp.arange(n_asg, dtype=jnp.int32) // TOP_K
    order = jnp.argsort(flat_e)
    se = flat_e[order]
    counts = jnp.bincount(flat_e, length=N_EXPERTS)
    padded = (counts + MOE_BLOCK - 1) // MOE_BLOCK * MOE_BLOCK
    pad_end = jnp.cumsum(padded)
    pad_start = pad_end - padded
    start = jnp.cumsum(counts) - counts
    dest = pad_start[se] + jnp.arange(n_asg, dtype=jnp.int32) - start[se]
    n_blocks = -(-n_asg // MOE_BLOCK) + N_EXPERTS
    rows = n_blocks * MOE_BLOCK
    row_tok = jnp.full((rows,), n_tok, dtype=jnp.int32).at[dest].set(flat_tok[order])
    row_w = jnp.zeros((rows,), jnp.float32).at[dest].set(flat_w[order])
    blk_e = jnp.minimum(jnp.searchsorted(pad_end, jnp.arange(n_blocks, dtype=jnp.int32) * MOE_BLOCK, side='right'), N_EXPERTS - 1)
    h_pad = jnp.concatenate([hx, jnp.zeros((1, d), hx.dtype)], axis=0)
    xs = h_pad[row_tok].reshape(n_blocks, MOE_BLOCK, d)

    def expert(args):
        xb, e = args
        return (jax.nn.silu(xb @ w_gate_e[e]) * (xb @ w_up_e[e])) @ w_down_e[e]

    ys = lax.map(expert, (xs, blk_e)).reshape(rows, d)
    out = jnp.zeros((n_tok + 1, d), jnp.float32).at[row_tok].add(ys.astype(jnp.float32) * row_w[:, None])
    return out[:n_tok].astype(hx.dtype)


def _decoder_layer(x, pos, mstate, kv_past, w_in, b_if, norm_mix, mlstm_norm, attn_sinks,
                   w_proj_a, w_proj_b, w_out, norm_ffn, w_group, b_group, w_router, b_router,
                   w_gate_e, w_up_e, w_down_e):
    b, t, _ = x.shape
    h = _rmsnorm(x, norm_mix)
    z = h @ w_in
    cuts = [int(c) for c in np.cumsum(SPLITS)[:-1]]
    mq, mk, mv, mo, mif, aq, ak, av, ga, gb = jnp.split(z, cuts, axis=-1)
    gates = mif.astype(jnp.float32) + b_if.astype(jnp.float32)
    ig = gates[..., :M_HEADS]
    logf = jax.nn.log_sigmoid(gates[..., M_HEADS:])
    heads = lambda a, nh, hd: a.reshape(b, t, nh, hd)
    hm, c_new, n_new, m_new = _mlstm(heads(mq, M_HEADS, M_HEAD_DIM),
                                     heads(mk, M_HEADS, M_HEAD_DIM) * (M_HEAD_DIM ** -0.5),
                                     heads(mv, M_HEADS, M_HEAD_DIM), ig, logf, *mstate)
    hm = hm * lax.rsqrt(jnp.mean(hm * hm, axis=-1, keepdims=True) + EPS) * mlstm_norm.astype(jnp.float32).reshape(M_HEADS, M_HEAD_DIM)
    hm = (hm.reshape(b, t, M_WIDTH) * jax.nn.sigmoid(mo.astype(jnp.float32))).astype(x.dtype)
    branch_a = hm @ w_proj_a
    q = _rope_partial(heads(aq, A_HEADS, A_HEAD_DIM), pos)
    k = _rope_partial(heads(ak, A_KV_HEADS, A_HEAD_DIM), pos)
    v = heads(av, A_KV_HEADS, A_HEAD_DIM)
    if kv_past is None:
        o = _swa_prompt(q, k, v, attn_sinks)
        new_k, new_v = k[:, -WINDOW:], v[:, -WINDOW:]
    else:
        ck, cv = kv_past
        wc = ck.shape[1]
        k_all = jnp.concatenate([ck.astype(k.dtype), k], axis=1)
        v_all = jnp.concatenate([cv.astype(v.dtype), v], axis=1)
        kpos = jnp.concatenate([pos[0] - wc + jnp.arange(wc, dtype=jnp.int32), pos])
        o = _sink_attention(q.reshape(b, t, A_KV_HEADS, A_GROUP, A_HEAD_DIM), k_all, v_all, pos, kpos, attn_sinks)
        o = o.reshape(b, t, A_Q_WIDTH)
        new_k, new_v = k_all[:, -wc:], v_all[:, -wc:]
    branch_b = o @ w_proj_b
    mixed = jax.nn.sigmoid(ga) * branch_a + jax.nn.sigmoid(gb) * branch_b
    x = x + mixed @ w_out
    f = _hier_moe(_rmsnorm(x, norm_ffn).reshape(b * t, D_MODEL), w_group, b_group, w_router, b_router,
                  w_gate_e, w_up_e, w_down_e)
    x = x + f.reshape(b, t, D_MODEL)
    return x, (c_new, n_new, m_new), (new_k, new_v)


def setup_inputs(seed: int = 0) -> dict:
    key = jax.random.key(seed)
    ks = jax.random.split(key, 32)
    f32 = jnp.float32
    nrm = lambda k, shape, scale: jax.random.normal(k, shape, f32) * scale
    cache_rows = min(WINDOW, PAST_LEN)
    b_if = jnp.concatenate([
        nrm(ks[8], (DEPTH, M_HEADS), 0.1),
        jnp.broadcast_to(jnp.linspace(3.0, 6.0, M_HEADS, dtype=f32), (DEPTH, M_HEADS)) + nrm(ks[9], (DEPTH, M_HEADS), 0.1),
    ], axis=-1)
    return {
        'x_prompt': nrm(ks[0], (BATCH, SEQ, D_MODEL), 1.0),
        'x_sample': nrm(ks[1], (DEC_BATCH, DEC_SEQ, D_MODEL), 1.0),
        'state_C': nrm(ks[2], (DEPTH, DEC_BATCH, M_HEADS, M_HEAD_DIM, M_HEAD_DIM), 0.05),
        'state_n': nrm(ks[3], (DEPTH, DEC_BATCH, M_HEADS, M_HEAD_DIM), 0.05),
        'state_m': nrm(ks[4], (DEPTH, DEC_BATCH, M_HEADS), 1.0),
        'cache_k': nrm(ks[5], (DEPTH, DEC_BATCH, cache_rows, A_KV_HEADS, A_HEAD_DIM), 1.0),
        'cache_v': nrm(ks[6], (DEPTH, DEC_BATCH, cache_rows, A_KV_HEADS, A_HEAD_DIM), 1.0),
        'w_in': nrm(ks[7], (DEPTH, D_MODEL, IN_WIDTH), D_MODEL ** -0.5),
        'b_if': b_if,
        'norm_mix': 1.0 + nrm(ks[10], (DEPTH, D_MODEL), 0.05),
        'mlstm_norm': 1.0 + nrm(ks[11], (DEPTH, M_WIDTH), 0.05),
        'attn_sinks': nrm(ks[12], (DEPTH, A_HEADS), 0.5),
        'w_proj_a': nrm(ks[13], (DEPTH, M_WIDTH, D_MODEL), M_WIDTH ** -0.5),
        'w_proj_b': nrm(ks[14], (DEPTH, A_Q_WIDTH, D_MODEL), A_Q_WIDTH ** -0.5),
        'w_out': nrm(ks[15], (DEPTH, D_MODEL, D_MODEL), D_MODEL ** -0.5),
        'norm_ffn': 1.0 + nrm(ks[16], (DEPTH, D_MODEL), 0.05),
        'w_group': nrm(ks[17], (DEPTH, D_MODEL, N_GROUPS), D_MODEL ** -0.5),
        'b_group': nrm(ks[18], (DEPTH, N_GROUPS), 0.01),
        'w_router': nrm(ks[19], (DEPTH, D_MODEL, N_EXPERTS), D_MODEL ** -0.5),
        'b_router': nrm(ks[20], (DEPTH, N_EXPERTS), 0.01),
        'w_gate_e': nrm(ks[21], (DEPTH, N_EXPERTS, D_MODEL, EXPERT_FF), D_MODEL ** -0.5),
        'w_up_e': nrm(ks[22], (DEPTH, N_EXPERTS, D_MODEL, EXPERT_FF), D_MODEL ** -0.5),
        'w_down_e': nrm(ks[23], (DEPTH, N_EXPERTS, EXPERT_FF, D_MODEL), EXPERT_FF ** -0.5),
        'norm_final': 1.0 + nrm(ks[24], (D_MODEL,), 0.05),
    }


def reference(x_prompt, x_sample, state_C, state_n, state_m, cache_k, cache_v, w_in, b_if, norm_mix,
              mlstm_norm, attn_sinks, w_proj_a, w_proj_b, w_out, norm_ffn, w_group, b_group, w_router,
              b_router, w_gate_e, w_up_e, w_down_e, norm_final):
    bp, tp = x_prompt.shape[:2]
    ts = x_sample.shape[1]
    pos_p = jnp.arange(tp, dtype=jnp.int32)
    pos_s = PAST_LEN + jnp.arange(ts, dtype=jnp.int32)
    xp, xs = x_prompt, x_sample
    pc, pn, pm, pk, pv = [], [], [], [], []
    sc, sn, sm, sk, sv = [], [], [], [], []
    for layer in range(DEPTH):
        lw = (w_in[layer], b_if[layer], norm_mix[layer], mlstm_norm[layer], attn_sinks[layer],
              w_proj_a[layer], w_proj_b[layer], w_out[layer], norm_ffn[layer], w_group[layer],
              b_group[layer], w_router[layer], b_router[layer], w_gate_e[layer], w_up_e[layer],
              w_down_e[layer])
        empty = (jnp.zeros((bp, M_HEADS, M_HEAD_DIM, M_HEAD_DIM), jnp.float32),
                 jnp.zeros((bp, M_HEADS, M_HEAD_DIM), jnp.float32),
                 jnp.zeros((bp, M_HEADS), jnp.float32))
        xp, (c1, n1, m1), (k1, v1) = _decoder_layer(xp, pos_p, empty, None, *lw)
        xs, (c2, n2, m2), (k2, v2) = _decoder_layer(
            xs, pos_s, (state_C[layer], state_n[layer], state_m[layer]), (cache_k[layer], cache_v[layer]), *lw)
        pc.append(c1); pn.append(n1); pm.append(m1); pk.append(k1); pv.append(v1)
        sc.append(c2); sn.append(n2); sm.append(m2); sk.append(k2); sv.append(v2)
    y_prompt = _rmsnorm(xp, norm_final)
    y_sample = _rmsnorm(xs, norm_final)
    p_C = jnp.stack(pc).astype(state_C.dtype)
    p_n = jnp.stack(pn).astype(state_n.dtype)
    p_m = jnp.stack(pm).astype(state_m.dtype)
    p_k = jnp.stack(pk).astype(cache_k.dtype)
    p_v = jnp.stack(pv).astype(cache_v.dtype)
    s_C = jnp.stack(sc).astype(state_C.dtype)
    s_n = jnp.stack(sn).astype(state_n.dtype)
    s_m = jnp.stack(sm).astype(state_m.dtype)
    s_k = jnp.stack(sk).astype(cache_k.dtype)
    s_v = jnp.stack(sv).astype(cache_v.dtype)
    return (y_prompt, y_sample, p_C, p_n, p_m, p_k, p_v, s_C, s_n, s_m, s_k, s_v)
```

```python
import functools

import jax
import jax.numpy as jnp
from jax import lax
from jax.experimental import pallas as pl
from jax.experimental.pallas import tpu as pltpu

F32 = jnp.float32
BF16 = jnp.bfloat16
NEG_INF = float("-inf")
PAD_LOG_GATE = -1e30

D_MODEL = 1024
M_HEADS = 4
M_HEAD_DIM = 256
A_HEADS = 16
A_KV_HEADS = 4
A_GROUP = 4
A_HEAD_DIM = 64
WINDOW = 128
ROPE_DIM = 16
ROPE_THETA = 500000.0
PAST_LEN = 16384
N_GROUPS = 4
EXPERTS_PER_GROUP = 8
N_EXPERTS = 32
EXPERT_FF = 256
EPS = 1e-6
K_SCALE = M_HEAD_DIM ** -0.5
ATTN_SCALE = A_HEAD_DIM ** -0.5

LANES = 128
SUBLANES = 8
VMEM_LIMIT = 56 * 1024 * 1024

C_MQ, C_MK, C_MV, C_MO = 0, 1024, 2048, 3072
C_AQ, C_AK, C_AV = 4096, 5120, 5376
C_GA, C_GB, C_IF = 5632, 6656, 7680
W_CAT = C_IF + LANES

TM_PROJ = 512
M_CHUNK = 128
MOE_BLOCK = 256
T_ROWS = 256


def _params(sem, vmem=VMEM_LIMIT):
    return pltpu.CompilerParams(dimension_semantics=sem, vmem_limit_bytes=vmem)


def _split3(x):
    hi = x.astype(BF16)
    r1 = x - hi.astype(F32)
    mid = r1.astype(BF16)
    lo = (r1 - mid.astype(F32)).astype(BF16)
    return hi, mid, lo


def _inproj_kernel(x_ref, nw_ref, w_ref, bif_ref, cos_ref, sin_ref,
                   mq_ref, mk_ref, mv_ref, mo_ref, aq_ref, ak_ref, av_ref, ga_ref, gb_ref, gt_ref):
    x = x_ref[...]
    h = x * lax.rsqrt(jnp.mean(x * x, axis=-1, keepdims=True) + EPS) * nw_ref[...]
    hb = h.astype(BF16)

    def proj(c0, width):
        return jnp.dot(hb, w_ref[:, c0:c0 + width], preferred_element_type=F32)

    mq_ref[...] = proj(C_MQ, 1024).astype(mq_ref.dtype)
    mk_ref[...] = (proj(C_MK, 1024) * K_SCALE).astype(mk_ref.dtype)
    mv_ref[...] = proj(C_MV, 1024).astype(mv_ref.dtype)
    mo_ref[...] = proj(C_MO, 1024).astype(mo_ref.dtype)
    ga_ref[...] = proj(C_GA, 1024).astype(ga_ref.dtype)
    gb_ref[...] = proj(C_GB, 1024).astype(gb_ref.dtype)
    av_ref[...] = proj(C_AV, 256).astype(av_ref.dtype)

    g = proj(C_IF, LANES) + bif_ref[...]
    lane = lax.broadcasted_iota(jnp.int32, g.shape, 1)
    logsig = jnp.minimum(g, 0.0) - jnp.log1p(jnp.exp(-jnp.abs(g)))
    gt_ref[...] = jnp.where(lane < M_HEADS, g, jnp.where(lane < 2 * M_HEADS, logsig, 0.0))

    cos_t = cos_ref[...]
    sin_t = sin_ref[...]
    lane_h = lax.broadcasted_iota(jnp.int32, cos_t.shape, 1) % A_HEAD_DIM
    low = lane_h < ROPE_DIM // 2
    in_rope = lane_h < ROPE_DIM

    def rope(y, out_ref):
        for j in range(y.shape[1] // LANES):
            xc = y[:, j * LANES:(j + 1) * LANES]
            partner = jnp.where(low, pltpu.roll(xc, LANES - ROPE_DIM // 2, axis=1),
                                pltpu.roll(xc, ROPE_DIM // 2, axis=1))
            r = xc * cos_t + partner * sin_t
            out_ref[:, j * LANES:(j + 1) * LANES] = jnp.where(in_rope, r, xc).astype(out_ref.dtype)

    rope(proj(C_AQ, 1024), aq_ref)
    rope(proj(C_AK, 256), ak_ref)


def _inproj(x2d, norm_w, w_cat, bif_row, cos_t, sin_t, *, tm, period, act_dtype):
    n = x2d.shape[0]
    const = lambda i: (0, 0)
    tile = lambda i: (i, 0)
    tab = lambda i: (i % period, 0)
    wide = lambda dt: jax.ShapeDtypeStruct((n, D_MODEL), dt)
    out_shape = (wide(act_dtype), wide(act_dtype), wide(act_dtype), wide(F32), wide(act_dtype),
                 jax.ShapeDtypeStruct((n, 256), F32), jax.ShapeDtypeStruct((n, 256), F32),
                 wide(F32), wide(F32), jax.ShapeDtypeStruct((n, LANES), F32))
    out_specs = tuple(pl.BlockSpec((tm, s.shape[1]), tile) for s in out_shape)
    return pl.pallas_call(
        _inproj_kernel,
        grid=(n // tm,),
        in_specs=[pl.BlockSpec((tm, D_MODEL), tile),
                  pl.BlockSpec((1, D_MODEL), const),
                  pl.BlockSpec((D_MODEL, W_CAT), const, pipeline_mode=pl.Buffered(1)),
                  pl.BlockSpec((1, LANES), const),
                  pl.BlockSpec((tm, LANES), tab),
                  pl.BlockSpec((tm, LANES), tab)],
        out_specs=out_specs,
        out_shape=out_shape,
        compiler_params=_params(("arbitrary",)),
        name="inproj",
    )(x2d, norm_w, w_cat, bif_row, cos_t, sin_t)


def _mlstm_kernel(q_ref, k_ref, v_ref, mo_ref, gt_ref, mn_ref, c0_ref, n0_ref, m0_ref,
                  hm_ref, c_ref, n_ref, m_ref, *, chunk):
    L = chunk

    @pl.when(pl.program_id(1) == 0)
    def _():
        c_ref[...] = c0_ref[...]
        n_ref[...] = n0_ref[...]
        m_ref[...] = m0_ref[...]

    gt = gt_ref[...]
    row = lax.broadcasted_iota(jnp.int32, (L, L), 0)
    col = lax.broadcasted_iota(jnp.int32, (L, L), 1)
    causal = col <= row
    cum = sum(jnp.dot(causal.astype(BF16), part, preferred_element_type=F32) for part in _split3(gt))
    if L % LANES == 0:
        gt_t = gt.T
        cum_t = cum.T
    else:
        eye = (lax.broadcasted_iota(jnp.int32, (LANES, LANES), 0)
               == lax.broadcasted_iota(jnp.int32, (LANES, LANES), 1)).astype(BF16)
        nt = (((1,), (1,)), ((), ()))
        gt_t = sum(lax.dot_general(eye, part, nt, preferred_element_type=F32) for part in _split3(gt))
        cum_t = sum(lax.dot_general(eye, part, nt, preferred_element_type=F32) for part in _split3(cum))

    for h in range(M_HEADS):
        hs = slice(h * M_HEAD_DIM, (h + 1) * M_HEAD_DIM)
        ig_row = gt_t[h:h + 1, :]
        ig_col = gt[:, h:h + 1]
        b_col = cum[:, M_HEADS + h:M_HEADS + h + 1]
        b_row = cum_t[M_HEADS + h:M_HEADS + h + 1, :]
        m_prev = m_ref[0, :, h:h + 1]

        dlog = jnp.where(causal, b_col - b_row + ig_row, NEG_INF)
        mt = jnp.maximum(b_col + m_prev, jnp.max(dlog, axis=-1, keepdims=True))
        dmat = jnp.exp(dlog - mt)
        sc = jnp.exp(b_col + m_prev - mt)

        q = q_ref[:, hs]
        k = k_ref[:, hs]
        v = v_ref[:, hs]
        qb, kb, vb = q.astype(BF16), k.astype(BF16), v.astype(BF16)
        qk = lax.dot_general(qb, kb, (((1,), (1,)), ((), ())), preferred_element_type=F32)
        s = qk * dmat
        cmat = c_ref[0, h]
        nvec = n_ref[0, h:h + 1, :]
        num = (jnp.dot(s.astype(BF16), vb, preferred_element_type=F32)
               + sc * jnp.dot(qb, cmat.astype(BF16), preferred_element_type=F32))
        den = (jnp.sum(s, axis=-1, keepdims=True)
               + sc * jnp.sum(q.astype(F32) * nvec, axis=-1, keepdims=True))
        hout = num / jnp.maximum(jnp.abs(den), jnp.exp(-mt))

        m_end = mt[L - 1:L, :]
        b_last = b_col[L - 1:L, :]
        ws = jnp.exp(b_last - b_col + ig_col - m_end)
        wc = jnp.exp(b_last + m_prev - m_end)
        kw = k.astype(F32) * ws
        c_ref[0, h] = wc * cmat + lax.dot_general(kw.astype(BF16), vb, (((0,), (0,)), ((), ())),
                                                  preferred_element_type=F32)
        n_ref[0, h:h + 1, :] = wc * nvec + jnp.sum(kw, axis=0, keepdims=True)
        m_ref[0, :, h:h + 1] = m_end

        hn = hout * lax.rsqrt(jnp.mean(hout * hout, axis=-1, keepdims=True) + EPS) * mn_ref[:, hs]
        hm_ref[:, hs] = (hn * jax.nn.sigmoid(mo_ref[:, hs].astype(F32))).astype(hm_ref.dtype)


def _mlstm(mq, mk, mv, mo, gt, mnorm, c0, n0, m0, *, batch, chunk, out_dtype):
    n = mq.shape[0]
    nc = n // (batch * chunk)
    tile = lambda b, c: (b * nc + c, 0)
    const = lambda b, c: (0, 0)
    st4 = lambda b, c: (b, 0, 0, 0)
    st3 = lambda b, c: (b, 0, 0)
    wide = pl.BlockSpec((chunk, D_MODEL), tile)
    out_shape = (jax.ShapeDtypeStruct((n, D_MODEL), out_dtype),
                 jax.ShapeDtypeStruct(c0.shape, F32),
                 jax.ShapeDtypeStruct(n0.shape, F32),
                 jax.ShapeDtypeStruct(m0.shape, F32))
    c_spec = pl.BlockSpec((1, M_HEADS, M_HEAD_DIM, M_HEAD_DIM), st4)
    n_spec = pl.BlockSpec((1, M_HEADS, M_HEAD_DIM), st3)
    m_spec = pl.BlockSpec((1, 1, M_HEADS), st3)
    return pl.pallas_call(
        functools.partial(_mlstm_kernel, chunk=chunk),
        grid=(batch, nc),
        in_specs=[wide, wide, wide, wide, pl.BlockSpec((chunk, LANES), tile),
                  pl.BlockSpec((1, D_MODEL), const), c_spec, n_spec, m_spec],
        out_specs=(wide, c_spec, n_spec, m_spec),
        out_shape=out_shape,
        compiler_params=_params(("arbitrary", "arbitrary")),
        name="mlstm",
    )(mq, mk, mv, mo, gt, mnorm, c0, n0, m0)


def _attn_kernel(sink_ref, q_ref, kc_ref, vc_ref, kp_ref, vp_ref, o_ref, *, tq, tkc, mask_first_prev):
    def keys(prev_ref, cur_ref):
        cur = cur_ref[...].astype(F32)
        if tkc < WINDOW:
            cur = jnp.concatenate([cur, jnp.zeros((WINDOW - tkc, cur.shape[1]), F32)], axis=0)
        return jnp.concatenate([prev_ref[...].astype(F32), cur], axis=0)

    kall = keys(kp_ref, kc_ref)
    vall = keys(vp_ref, vc_ref)

    rows = A_GROUP * tq
    t = lax.broadcasted_iota(jnp.int32, (rows, 2 * WINDOW), 0) % tq
    j = lax.broadcasted_iota(jnp.int32, (rows, 2 * WINDOW), 1)
    valid = (j > t) & (j <= t + WINDOW)
    if mask_first_prev:
        valid = valid & ((j >= WINDOW) | (pl.program_id(1) > 0))
    lane = lax.broadcasted_iota(jnp.int32, (2 * WINDOW, LANES), 1)
    head_of_lane = lax.broadcasted_iota(jnp.int32, (tq, A_GROUP * A_HEAD_DIM), 1) // A_HEAD_DIM

    def tile4(a, g):
        c = a[:, (g // 2) * LANES:(g // 2 + 1) * LANES]
        own_low = (lane < A_HEAD_DIM) if g % 2 == 0 else (lane >= A_HEAD_DIM)
        base = jnp.where(own_low, c, pltpu.roll(c, A_HEAD_DIM, axis=1))
        return jnp.concatenate([base, base], axis=1).astype(BF16)

    for g in range(A_KV_HEADS):
        k4 = tile4(kall, g)
        v4 = tile4(vall, g)
        qg = q_ref[:, g * 256:(g + 1) * 256].astype(BF16)
        zero = jnp.zeros_like(qg)
        qbd = jnp.concatenate([jnp.where(head_of_lane == h, qg, zero) for h in range(A_GROUP)], axis=0)
        s = lax.dot_general(qbd, k4, (((1,), (1,)), ((), ())), preferred_element_type=F32) * ATTN_SCALE
        s = jnp.where(valid, s, NEG_INF)
        sink = jnp.concatenate([jnp.full((tq, 1), sink_ref[g * A_GROUP + h], F32) for h in range(A_GROUP)], axis=0)
        m = jnp.maximum(jnp.max(s, axis=-1, keepdims=True), sink)
        p = jnp.exp(s - m)
        den = jnp.sum(p, axis=-1, keepdims=True) + jnp.exp(sink - m)
        probs = (p / den).astype(BF16)
        o4 = jnp.dot(probs, v4, preferred_element_type=F32)
        og = jnp.zeros((tq, 256), F32)
        for h in range(A_GROUP):
            og = og + jnp.where(head_of_lane == h, o4[h * tq:(h + 1) * tq], 0.0)
        o_ref[:, g * 256:(g + 1) * 256] = og.astype(o_ref.dtype)


def _attention(sinks, q2d, kcur, vcur, kprev, vprev, *, batch, nblk, tq, tkc, mask_first_prev, out_dtype):
    n = q2d.shape[0]
    cur = lambda b, i: (b * nblk + i, 0)
    prev = (lambda b, i: (b * nblk + jnp.maximum(i - 1, 0), 0)) if mask_first_prev else (lambda b, i: (b, 0))
    return pl.pallas_call(
        functools.partial(_attn_kernel, tq=tq, tkc=tkc, mask_first_prev=mask_first_prev),
        grid=(batch, nblk),
        in_specs=[pl.BlockSpec(memory_space=pltpu.SMEM),
                  pl.BlockSpec((tq, D_MODEL), cur),
                  pl.BlockSpec((tkc, 256), cur), pl.BlockSpec((tkc, 256), cur),
                  pl.BlockSpec((WINDOW, 256), prev), pl.BlockSpec((WINDOW, 256), prev)],
        out_specs=pl.BlockSpec((tq, D_MODEL), cur),
        out_shape=jax.ShapeDtypeStruct((n, D_MODEL), out_dtype),
        compiler_params=_params(("arbitrary", "arbitrary")),
        name="attn",
    )(sinks, q2d, kcur, vcur, kprev, vprev)


def _mix_kernel(x_ref, hm_ref, o_ref, ga_ref, gb_ref, wa_ref, wb_ref, wo_ref, nf_ref,
                wr_ref, br_ref,
                x1_ref, hx_ref, ri_ref, rw_ref, cnt_ref, base_ref):
    @pl.when(pl.program_id(0) == 0)
    def _():
        base_ref[...] = jnp.zeros_like(base_ref)

    a = jnp.dot(hm_ref[...].astype(BF16), wa_ref[...], preferred_element_type=F32)
    b = jnp.dot(o_ref[...].astype(BF16), wb_ref[...], preferred_element_type=F32)
    mixed = jax.nn.sigmoid(ga_ref[...]) * a + jax.nn.sigmoid(gb_ref[...]) * b
    x1 = x_ref[...] + jnp.dot(mixed.astype(BF16), wo_ref[...], preferred_element_type=F32)
    x1_ref[...] = x1
    hx = x1 * lax.rsqrt(jnp.mean(x1 * x1, axis=-1, keepdims=True) + EPS) * nf_ref[...]
    hx_ref[...] = hx.astype(hx_ref.dtype)

    tm = hx.shape[0]
    lane_i = lax.broadcasted_iota(jnp.int32, (tm, LANES), 1)
    lane = lane_i.astype(F32)
    big = float(LANES)
    hx_hi = hx.astype(BF16)
    hx_lo = (hx - hx_hi.astype(F32)).astype(BF16)
    r_hi = jnp.dot(hx_hi, wr_ref[...], preferred_element_type=F32)
    r_lo = jnp.dot(hx_lo, wr_ref[...], preferred_element_type=F32)
    logits = (r_hi[:, :LANES] + r_hi[:, LANES:]) + (r_lo[:, :LANES] + r_lo[:, LANES:]) + br_ref[...]
    gl = jnp.where((lane_i >= N_EXPERTS) & (lane_i < N_EXPERTS + N_GROUPS), logits, NEG_INF)
    el = jnp.where(lane_i < N_EXPERTS, logits, NEG_INF)

    lmax = lambda z: jnp.max(z, axis=-1, keepdims=True)
    lmin = lambda z: jnp.min(z, axis=-1, keepdims=True)
    lsum = lambda z: jnp.sum(z, axis=-1, keepdims=True)

    gmax = lmax(gl)
    grp = lmin(jnp.where(gl == gmax, lane, big)) - float(N_EXPERTS)
    p_grp = 1.0 / lsum(jnp.exp(gl - gmax))
    in_grp = jnp.floor(lane * (1.0 / EXPERTS_PER_GROUP)) == grp
    v1 = jnp.where(in_grp, el, NEG_INF)
    t1 = lmax(v1)
    i1 = lmin(jnp.where(v1 == t1, lane, big))
    v2 = jnp.where(lane == i1, NEG_INF, v1)
    t2 = lmax(v2)
    i2 = lmin(jnp.where(v2 == t2, lane, big))
    e2 = jnp.exp(t2 - t1)
    w1 = p_grp * (1.0 / (1.0 + e2))
    w2 = p_grp * (e2 / (1.0 + e2))

    oh1 = lane == i1
    oh2 = lane == i2
    cnt = oh1.astype(F32) + oh2.astype(F32)
    r = lax.broadcasted_iota(jnp.int32, (tm, tm), 0)
    c = lax.broadcasted_iota(jnp.int32, (tm, tm), 1)
    before = jnp.dot((c < r).astype(BF16), cnt.astype(BF16), preferred_element_type=F32)
    tot = before + base_ref[0:1, :]
    rank1 = lsum(jnp.where(oh1, tot, 0.0))
    rank2 = lsum(jnp.where(oh2, tot, 0.0))
    new_base = base_ref[0:1, :] + jnp.sum(cnt, axis=0, keepdims=True)
    base_ref[...] = jnp.broadcast_to(new_base, base_ref.shape)
    cnt_ref[...] = jnp.broadcast_to(new_base, cnt_ref.shape)

    info = jnp.where(lane_i == 0, i1, jnp.where(lane_i == 1, i2, jnp.where(lane_i == 2, rank1,
                     jnp.where(lane_i == 3, rank2, 0.0))))
    ri_ref[...] = info.astype(jnp.int32)
    rw_ref[...] = jnp.where(lane_i == 0, w1, jnp.where(lane_i == 1, w2, 0.0))


def _mix(x2d, hm, o, ga, gb, wa, wb, wo, nf, wr, br, *, tm):
    n = x2d.shape[0]
    tile = lambda i: (i, 0)
    const = lambda i: (0, 0)
    wide = pl.BlockSpec((tm, D_MODEL), tile)
    narrow = pl.BlockSpec((tm, LANES), tile)
    wsq = pl.BlockSpec((D_MODEL, D_MODEL), const)
    wr_spec = pl.BlockSpec((D_MODEL, 2 * LANES), const)
    row = pl.BlockSpec((1, LANES), const)
    out_shape = (jax.ShapeDtypeStruct((n, D_MODEL), F32),
                 jax.ShapeDtypeStruct((n, D_MODEL), F32),
                 jax.ShapeDtypeStruct((n, LANES), jnp.int32),
                 jax.ShapeDtypeStruct((n, LANES), F32),
                 jax.ShapeDtypeStruct((SUBLANES, LANES), F32))
    return pl.pallas_call(
        _mix_kernel,
        grid=(n // tm,),
        in_specs=[wide, wide, wide, wide, wide, wsq, wsq, wsq, pl.BlockSpec((1, D_MODEL), const),
                  wr_spec, row],
        out_specs=(wide, wide, narrow, narrow, pl.BlockSpec((SUBLANES, LANES), const)),
        out_shape=out_shape,
        scratch_shapes=[pltpu.VMEM((SUBLANES, LANES), F32)],
        compiler_params=_params(("arbitrary",)),
        name="mix",
    )(x2d, hm, o, ga, gb, wa, wb, wo, nf, wr, br)


def _row_copy(src_hbm, dst_hbm, src_row, dst_row, sem):
    return pltpu.make_async_copy(src_hbm.at[pl.ds(src_row, 1)], dst_hbm.at[pl.ds(dst_row, 1)], sem)


def _dispatch_kernel(dest_ref, hx_hbm, xs_in_hbm, xs_hbm, sem, *, rows):
    del xs_in_hbm
    base = pl.program_id(0) * rows

    def start(t, carry):
        _row_copy(hx_hbm, xs_hbm, base + t, dest_ref[0, 0, 2 * t], sem.at[0]).start()
        _row_copy(hx_hbm, xs_hbm, base + t, dest_ref[0, 0, 2 * t + 1], sem.at[0]).start()
        return carry

    lax.fori_loop(0, rows, start, 0)

    def wait(t, carry):
        _row_copy(hx_hbm, xs_hbm, 0, 0, sem.at[0]).wait()
        return carry

    lax.fori_loop(0, 2 * rows, wait, 0)


def _dispatch(dest3, hx, xs_zero, *, rows):
    n = hx.shape[0]
    return pl.pallas_call(
        functools.partial(_dispatch_kernel, rows=rows),
        grid=(n // rows,),
        in_specs=[pl.BlockSpec((1, 1, 2 * rows), lambda i: (i, 0, 0), memory_space=pltpu.SMEM),
                  pl.BlockSpec(memory_space=pl.ANY),
                  pl.BlockSpec(memory_space=pl.ANY)],
        out_specs=pl.BlockSpec(memory_space=pl.ANY),
        out_shape=jax.ShapeDtypeStruct(xs_zero.shape, xs_zero.dtype),
        scratch_shapes=[pltpu.SemaphoreType.DMA((1,))],
        input_output_aliases={2: 0},
        compiler_params=pltpu.CompilerParams(dimension_semantics=("arbitrary",), has_side_effects=True),
        name="dispatch",
    )(dest3, hx, xs_zero)


def _experts_kernel(blk_e_ref, nact_ref, xs_ref, w1_ref, w2_ref, y_ref):
    del blk_e_ref

    @pl.when(pl.program_id(0) < nact_ref[0])
    def _():
        gu = jnp.dot(xs_ref[...].astype(BF16), w1_ref[0], preferred_element_type=F32)
        gate = gu[:, :EXPERT_FF]
        up = gu[:, EXPERT_FF:]
        act = (gate * jax.nn.sigmoid(gate)) * up
        y_ref[...] = jnp.dot(act.astype(BF16), w2_ref[0], preferred_element_type=F32)

    @pl.when(pl.program_id(0) >= nact_ref[0])
    def _():
        y_ref[...] = jnp.zeros_like(y_ref)


def _experts(blk_e, nact, xs, w1, w2, *, block):
    rows = xs.shape[0]
    row_blk = lambda i, be, na: (jnp.minimum(i, na[0] - 1), 0)
    w_blk = lambda i, be, na: (be[i], 0, 0)
    return pl.pallas_call(
        _experts_kernel,
        grid_spec=pltpu.PrefetchScalarGridSpec(
            num_scalar_prefetch=2,
            grid=(rows // block,),
            in_specs=[pl.BlockSpec((block, D_MODEL), row_blk),
                      pl.BlockSpec((1, D_MODEL, 2 * EXPERT_FF), w_blk),
                      pl.BlockSpec((1, EXPERT_FF, D_MODEL), w_blk)],
            out_specs=pl.BlockSpec((block, D_MODEL), lambda i, be, na: (i, 0))),
        out_shape=jax.ShapeDtypeStruct((rows, D_MODEL), F32),
        compiler_params=_params(("arbitrary",)),
        name="experts",
    )(blk_e, nact, xs, w1, w2)


def _combine_kernel(dcur_ref, dnext_ref, x1_ref, rw_ref, nw_ref, y_hbm, out_ref, ybuf, sem, *, rows):
    i = pl.program_id(0)
    nsteps = pl.num_programs(0)
    slot = i % 2

    def gather(dref, to_slot):
        def start(t, carry):
            pltpu.make_async_copy(y_hbm.at[pl.ds(dref[0, 0, 2 * t], 1)],
                                  ybuf.at[to_slot, 0, pl.ds(t, 1)], sem.at[to_slot]).start()
            pltpu.make_async_copy(y_hbm.at[pl.ds(dref[0, 0, 2 * t + 1], 1)],
                                  ybuf.at[to_slot, 1, pl.ds(t, 1)], sem.at[to_slot]).start()
            return carry
        lax.fori_loop(0, rows, start, 0)

    @pl.when(i == 0)
    def _():
        gather(dcur_ref, 0)

    @pl.when(i + 1 < nsteps)
    def _():
        gather(dnext_ref, 1 - slot)

    def wait(t, carry):
        pltpu.make_async_copy(y_hbm.at[pl.ds(0, 1)], ybuf.at[slot, 0, pl.ds(0, 1)], sem.at[slot]).wait()
        return carry

    lax.fori_loop(0, 2 * rows, wait, 0)

    rw = rw_ref[...]
    f = ybuf[slot, 0] * rw[:, 0:1] + ybuf[slot, 1] * rw[:, 1:2]
    x2 = x1_ref[...] + f
    out_ref[...] = x2 * lax.rsqrt(jnp.mean(x2 * x2, axis=-1, keepdims=True) + EPS) * nw_ref[...]


def _combine(dest3, x1, rw, nfin, y, *, rows):
    n = x1.shape[0]
    nsteps = n // rows
    tile = lambda i: (i, 0)
    dspec = lambda f: pl.BlockSpec((1, 1, 2 * rows), f, memory_space=pltpu.SMEM)
    return pl.pallas_call(
        functools.partial(_combine_kernel, rows=rows),
        grid=(nsteps,),
        in_specs=[dspec(lambda i: (i, 0, 0)),
                  dspec(lambda i: (jnp.minimum(i + 1, nsteps - 1), 0, 0)),
                  pl.BlockSpec((rows, D_MODEL), tile),
                  pl.BlockSpec((rows, LANES), tile),
                  pl.BlockSpec((1, D_MODEL), lambda i: (0, 0)),
                  pl.BlockSpec(memory_space=pl.ANY)],
        out_specs=pl.BlockSpec((rows, D_MODEL), tile),
        out_shape=jax.ShapeDtypeStruct((n, D_MODEL), F32),
        scratch_shapes=[pltpu.VMEM((2, 2, rows, D_MODEL), F32), pltpu.SemaphoreType.DMA((2,))],
        compiler_params=_params(("arbitrary",)),
        name="combine",
    )(dest3, dest3, x1, rw, nfin, y)


def _rope_tables(pos):
    inv_freq = jnp.power(jnp.float32(ROPE_THETA), -jnp.arange(0, ROPE_DIM, 2, dtype=F32) / ROPE_DIM)
    ang = pos.astype(F32)[:, None] * inv_freq[None, :]
    lane_h = jnp.arange(LANES) % A_HEAD_DIM
    idx = lane_h % (ROPE_DIM // 2)
    cos = jnp.cos(ang)[:, idx]
    sin = jnp.sin(ang)[:, idx]
    cos_t = jnp.where(lane_h < ROPE_DIM, cos, 1.0)
    sin_t = jnp.where(lane_h < ROPE_DIM // 2, -sin, jnp.where(lane_h < ROPE_DIM, sin, 0.0))
    return cos_t, sin_t


def _moe_and_final(x1, hx, ri, rw, cnt, w1, w2, nfin):
    n = x1.shape[0]
    rows_step = min(T_ROWS, n)
    n_blocks = -(-(2 * n) // MOE_BLOCK) + N_EXPERTS
    counts = cnt[0, :N_EXPERTS].astype(jnp.int32)
    padded = (counts + MOE_BLOCK - 1) // MOE_BLOCK * MOE_BLOCK
    pad_end = jnp.cumsum(padded)
    pad_start = pad_end - padded
    dest = pad_start[ri[:, 0:2]] + ri[:, 2:4]
    dest3 = dest.reshape(n // rows_step, 1, 2 * rows_step)
    nact = (pad_end[-1] // MOE_BLOCK).astype(jnp.int32)
    blk = jnp.minimum(jnp.arange(n_blocks, dtype=jnp.int32), nact - 1) * MOE_BLOCK
    blk_e = jnp.minimum(jnp.searchsorted(pad_end, blk, side="right"), N_EXPERTS - 1).astype(jnp.int32)

    xs = _dispatch(dest3, hx, jnp.zeros((n_blocks * MOE_BLOCK, D_MODEL), F32), rows=rows_step)
    y = _experts(blk_e, nact.reshape(1), xs, w1, w2, block=MOE_BLOCK)
    return _combine(dest3, x1, rw, nfin, y, rows=rows_step)


def kernel(x_prompt, x_sample, state_C, state_n, state_m, cache_k, cache_v, w_in, b_if, norm_mix, mlstm_norm,
           attn_sinks, w_proj_a, w_proj_b, w_out, norm_ffn, w_group, b_group, w_router, b_router, w_gate_e,
           w_up_e, w_down_e, norm_final):
    bp, tp, _ = x_prompt.shape
    bs, ts, _ = x_sample.shape
    assert w_in.shape[0] == 1 and ts <= SUBLANES and tp % WINDOW == 0 and tp % M_CHUNK == 0
    ts_pad = SUBLANES

    wi = w_in[0]
    w_cat = jnp.concatenate([wi[:, :4096], wi[:, 4096 + 2 * M_HEADS:], wi[:, 4096:4096 + 2 * M_HEADS],
                             jnp.zeros((D_MODEL, LANES - 2 * M_HEADS), F32)], axis=1).astype(BF16)
    bif_row = jnp.pad(b_if[0], (0, LANES - 2 * M_HEADS)).reshape(1, LANES)
    nmix = norm_mix[0].reshape(1, D_MODEL)
    mnorm = mlstm_norm[0].reshape(1, D_MODEL)
    nffn = norm_ffn[0].reshape(1, D_MODEL)
    nfin = norm_final.reshape(1, D_MODEL)
    sinks = attn_sinks[0]
    wa, wb, wo = w_proj_a[0].astype(BF16), w_proj_b[0].astype(BF16), w_out[0].astype(BF16)
    n_route = N_EXPERTS + N_GROUPS
    wr32 = jnp.pad(jnp.concatenate([w_router[0], w_group[0]], axis=1), ((0, 0), (0, LANES - n_route)))
    wr_hi = wr32.astype(BF16)
    wr_lo = (wr32 - wr_hi.astype(F32)).astype(BF16)
    wr = jnp.concatenate([wr_hi, wr_lo], axis=1)
    br = jnp.pad(jnp.concatenate([b_router[0], b_group[0]]), (0, LANES - n_route)).reshape(1, LANES)
    w1 = jnp.concatenate([w_gate_e[0], w_up_e[0]], axis=-1).astype(BF16)
    w2 = w_down_e[0].astype(BF16)

    def tail(x2d, hm, o, ga, gb):
        tm = min(TM_PROJ, x2d.shape[0])
        x1, hx, ri, rw, cnt = _mix(x2d, hm, o, ga, gb, wa, wb, wo, nffn, wr, br, tm=tm)
        return _moe_and_final(x1, hx, ri, rw, cnt, w1, w2, nfin)

    n_p = bp * tp
    tm_p = min(TM_PROJ, tp)
    cos_p, sin_p = _rope_tables(jnp.arange(tp, dtype=jnp.int32))
    xp2d = x_prompt.reshape(n_p, D_MODEL)
    mq, mk, mv, mo, aq, ak, av, ga, gb, gt = _inproj(xp2d, nmix, w_cat, bif_row, cos_p, sin_p,
                                                     tm=tm_p, period=tp // tm_p, act_dtype=BF16)
    zc = jnp.zeros((bp, M_HEADS, M_HEAD_DIM, M_HEAD_DIM), F32)
    zn = jnp.zeros((bp, M_HEADS, M_HEAD_DIM), F32)
    zm = jnp.zeros((bp, 1, M_HEADS), F32)
    hm, p_c, p_n, p_m = _mlstm(mq, mk, mv, mo, gt, mnorm, zc, zn, zm, batch=bp, chunk=M_CHUNK, out_dtype=BF16)
    o = _attention(sinks, aq, ak, av, ak, av, batch=bp, nblk=tp // WINDOW, tq=WINDOW, tkc=WINDOW,
                   mask_first_prev=True, out_dtype=BF16)
    y_prompt = tail(xp2d, hm, o, ga, gb).reshape(bp, tp, D_MODEL)
    p_k = ak.reshape(bp, tp, A_KV_HEADS, A_HEAD_DIM)[:, tp - WINDOW:]
    p_v = av.reshape(bp, tp, A_KV_HEADS, A_HEAD_DIM)[:, tp - WINDOW:]

    n_s = bs * ts
    pos_s = PAST_LEN + jnp.arange(ts, dtype=jnp.int32)
    cos_s, sin_s = _rope_tables(jnp.tile(pos_s, bs))
    xs2d = x_sample.reshape(n_s, D_MODEL)
    smq, smk, smv, smo, saq, sak, sav, sga, sgb, sgt = _inproj(xs2d, nmix, w_cat, bif_row, cos_s, sin_s,
                                                               tm=n_s, period=1, act_dtype=F32)

    def pad_t(a, value=0.0):
        a3 = a.reshape(bs, ts, a.shape[-1])
        a3 = jnp.pad(a3, ((0, 0), (0, ts_pad - ts), (0, 0)), constant_values=value)
        return a3.reshape(bs * ts_pad, a.shape[-1])

    def unpad_t(a):
        return a.reshape(bs, ts_pad, a.shape[-1])[:, :ts].reshape(n_s, a.shape[-1])

    lane = jnp.arange(LANES)
    sgt3 = sgt.reshape(bs, ts, LANES)
    pad_row = jnp.where(lane < M_HEADS, PAD_LOG_GATE, 0.0).astype(F32)
    sgt_p = jnp.concatenate([sgt3, jnp.broadcast_to(pad_row, (bs, ts_pad - ts, LANES))], axis=1)
    sgt_p = sgt_p.reshape(bs * ts_pad, LANES)
    shm, s_c, s_n, s_m = _mlstm(pad_t(smq), pad_t(smk), pad_t(smv), pad_t(smo), sgt_p, mnorm,
                                state_C[0], state_n[0], state_m[0].reshape(bs, 1, M_HEADS),
                                batch=bs, chunk=ts_pad, out_dtype=F32)
    ck = cache_k[0].reshape(bs * WINDOW, A_KV_HEADS * A_HEAD_DIM)
    cv = cache_v[0].reshape(bs * WINDOW, A_KV_HEADS * A_HEAD_DIM)
    so = _attention(sinks, pad_t(saq), pad_t(sak), pad_t(sav), ck, cv, batch=bs, nblk=1, tq=ts_pad, tkc=ts_pad,
                    mask_first_prev=False, out_dtype=F32)
    y_sample = tail(xs2d, unpad_t(shm), unpad_t(so), sga, sgb).reshape(bs, ts, D_MODEL)
    s_k = jnp.concatenate([cache_k[0], sak.reshape(bs, ts, A_KV_HEADS, A_HEAD_DIM)], axis=1)[:, -WINDOW:]
    s_v = jnp.concatenate([cache_v[0], sav.reshape(bs, ts, A_KV_HEADS, A_HEAD_DIM)], axis=1)[:, -WINDOW:]

    return (y_prompt, y_sample, p_c[None], p_n[None], p_m.reshape(1, bp, M_HEADS), p_k[None], p_v[None],
            s_c[None], s_n[None], s_m.reshape(1, bs, M_HEADS), s_k[None], s_v[None])
```
